```python
import jax, jax.numpy as jnp
from jax import lax
import numpy as np

D_MODEL = 1024
BATCH = 8
SEQ = 2048
DEPTH = 4

GRID_W = 64
CTX_LEN = 256
N_FOURIER_GROUPS = 4
FOURIER_GROUP_DIM = D_MODEL // 8
FOURIER_WIDTH = N_FOURIER_GROUPS * FOURIER_GROUP_DIM
N_SGU_HEADS = 4
SGU_HEAD_DIM = D_MODEL // 8
SGU_WIDTH = N_SGU_HEADS * SGU_HEAD_DIM
CHUNK = 128
EVEN_IN_WIDTH = FOURIER_WIDTH + 2 * SGU_WIDTH
EVEN_OUT_WIDTH = FOURIER_WIDTH + SGU_WIDTH
NA_HEADS = 16
NA_HEAD_DIM = D_MODEL // NA_HEADS
WIN_H = 8
WIN_W = 16
D_FF = 2816
CONV_W = 3
N_MOD = 6
EPS = 1e-6
NEG_INF = -1e30
N_EVEN = (DEPTH + 1) // 2
N_ODD = DEPTH // 2

kernel_name = "hybrid_fourier_sgu_natten_dit"


def rms_norm(x, g=None):
    xf = x.astype(jnp.float32)
    y = xf * lax.rsqrt(jnp.mean(xf * xf, axis=-1, keepdims=True) + EPS)
    if g is not None:
        y = y * g.astype(jnp.float32)
    return y.astype(x.dtype)


def modulate(x, g, shift, scale):
    return rms_norm(x, g) * (1 + scale) + shift


def fourier_mix(f):
    y = jnp.fft.fftn(f.astype(jnp.float32), axes=(1, 3), norm="ortho").real
    return y.astype(f.dtype)


def chunk_sgu(u, v, w_s, b_s):
    B_, L, G, C = v.shape
    vc = rms_norm(v).reshape(B_, L // CHUNK, CHUNK, G, C)
    s = jnp.einsum('gpq,bnqgc->bnpgc', w_s, vc) + b_s.T[None, None, :, :, None]
    return u * s.reshape(B_, L, G, C)


def even_mixer(h, w_in, w_s, b_s, w_out):
    B_, L, _ = h.shape
    p = h @ w_in
    f = p[..., :FOURIER_WIDTH].reshape(B_, L, N_FOURIER_GROUPS, FOURIER_GROUP_DIM)
    uv = jax.nn.gelu(p[..., FOURIER_WIDTH:])
    u = uv[..., :SGU_WIDTH].reshape(B_, L, N_SGU_HEADS, SGU_HEAD_DIM)
    v = uv[..., SGU_WIDTH:].reshape(B_, L, N_SGU_HEADS, SGU_HEAD_DIM)
    a = fourier_mix(f).reshape(B_, L, FOURIER_WIDTH)
    b = chunk_sgu(u, v, w_s, b_s).reshape(B_, L, SGU_WIDTH)
    return jnp.concatenate([a, b], axis=-1) @ w_out


def _heads(t):
    B_, L, _ = t.shape
    return t.reshape(B_, L, NA_HEADS, NA_HEAD_DIM).transpose(0, 2, 1, 3)


def _merge(o):
    B_, H, L, d = o.shape
    return o.transpose(0, 2, 1, 3).reshape(B_, L, H * d)


def qkv_heads(h, w_qkv, q_g, k_g):
    q, k, v = jnp.split(h @ w_qkv, 3, axis=-1)
    return rms_norm(_heads(q), q_g), rms_norm(_heads(k), k_g), _heads(v)


def neighbourhood_attention(q, k, v, k_c, v_c, rpb):
    B_, H, L, d = q.shape
    rows = L // GRID_W
    kh = min(WIN_H, rows)
    r = jnp.arange(rows)
    row_start = jnp.clip(r - kh // 2, 0, rows - kh)
    row_idx = row_start[:, None] + jnp.arange(kh)[None, :]
    col = jnp.arange(GRID_W)
    col_start = jnp.clip(col - WIN_W // 2, 0, GRID_W - WIN_W)
    col_mask = (col[None, :] >= col_start[:, None]) & (col[None, :] < col_start[:, None] + WIN_W)
    dr = row_idx - r[:, None] + (WIN_H - 1)
    dc = jnp.clip(col[None, :] - col[:, None] + (WIN_W - 1), 0, 2 * WIN_W - 2)
    bias = rpb[:, dr][:, :, :, dc].transpose(0, 1, 3, 2, 4)
    scale = d ** -0.5
    qg = q.reshape(B_, H, rows, GRID_W, d)
    kg = k.reshape(B_, H, rows, GRID_W, d)[:, :, row_idx]
    vg = v.reshape(B_, H, rows, GRID_W, d)[:, :, row_idx]
    s_loc = jnp.einsum('bhrqd,bhrikd->bhrqik', qg, kg).astype(jnp.float32) * scale + bias
    s_loc = jnp.where(col_mask[:, None, :], s_loc, NEG_INF)
    s_ctx = jnp.einsum('bhrqd,bhcd->bhrqc', qg, k_c).astype(jnp.float32) * scale
    n_loc = kh * GRID_W
    s = jnp.concatenate([s_loc.reshape(B_, H, rows, GRID_W, n_loc), s_ctx], axis=-1)
    p = jax.nn.softmax(s, axis=-1).astype(v.dtype)
    p_loc = p[..., :n_loc].reshape(B_, H, rows, GRID_W, kh, GRID_W)
    p_ctx = p[..., n_loc:]
    o = jnp.einsum('bhrqik,bhrikd->bhrqd', p_loc, vg) + jnp.einsum('bhrqc,bhcd->bhrqd', p_ctx, v_c)
    return o.reshape(B_, H, L, d)


def context_attention(q_c, k_c, v_c):
    s = jnp.einsum('bhqd,bhkd->bhqk', q_c, k_c).astype(jnp.float32) * (q_c.shape[-1] ** -0.5)
    p = jax.nn.softmax(s, axis=-1).astype(v_c.dtype)
    return jnp.einsum('bhqk,bhkd->bhqd', p, v_c)


def conv_ffn(h, w_up, conv_w, conv_b, w_down):
    L = h.shape[1]
    a = h @ w_up
    gate, val = a[..., :D_FF], a[..., D_FF:]
    gp = jnp.pad(gate, ((0, 0), (CONV_W // 2, CONV_W // 2), (0, 0)))
    g = gp[:, 0:L] * conv_w[0] + gp[:, 1:1 + L] * conv_w[1] + gp[:, 2:2 + L] * conv_w[2] + conv_b
    return (jax.nn.silu(g) * val) @ w_down


def setup_inputs(seed: int = 0) -> dict:
    key = jax.random.key(seed)
    ks = jax.random.split(key, 24)
    nrm = jax.random.normal
    D = D_MODEL
    return {
        "x": nrm(ks[0], (BATCH, SEQ, D), jnp.float32),
        "c": nrm(ks[1], (BATCH, D), jnp.float32),
        "ctx": nrm(ks[2], (BATCH, CTX_LEN, D), jnp.float32),
        "c_ctx": nrm(ks[3], (D,), jnp.float32),
        "norm1_g": 1.0 + 0.1 * nrm(ks[4], (DEPTH, D), jnp.float32),
        "norm2_g": 1.0 + 0.1 * nrm(ks[5], (DEPTH, D), jnp.float32),
        "ada_w": 0.5 * D ** -0.5 * nrm(ks[6], (DEPTH, D, N_MOD * D), jnp.float32),
        "ada_b": 0.02 * nrm(ks[7], (DEPTH, N_MOD * D), jnp.float32),
        "even_w_in": D ** -0.5 * nrm(ks[8], (N_EVEN, D, EVEN_IN_WIDTH), jnp.float32),
        "even_w_s": CHUNK ** -0.5 * nrm(ks[9], (N_EVEN, N_SGU_HEADS, CHUNK, CHUNK), jnp.float32),
        "even_b_s": 1.0 + 0.1 * nrm(ks[10], (N_EVEN, N_SGU_HEADS, CHUNK), jnp.float32),
        "even_w_out": EVEN_OUT_WIDTH ** -0.5 * nrm(ks[11], (N_EVEN, EVEN_OUT_WIDTH, D), jnp.float32),
        "odd_w_qkv": D ** -0.5 * nrm(ks[12], (N_ODD, D, 3 * D), jnp.float32),
        "odd_q_g": 1.0 + 0.1 * nrm(ks[13], (N_ODD, NA_HEAD_DIM), jnp.float32),
        "odd_k_g": 1.0 + 0.1 * nrm(ks[14], (N_ODD, NA_HEAD_DIM), jnp.float32),
        "odd_rpb": 0.1 * nrm(ks[15], (N_ODD, NA_HEADS, 2 * WIN_H - 1, 2 * WIN_W - 1), jnp.float32),
        "odd_w_o": D ** -0.5 * nrm(ks[16], (N_ODD, D, D), jnp.float32),
        "ffn_w_up": D ** -0.5 * nrm(ks[17], (DEPTH, D, 2 * D_FF), jnp.float32),
        "ffn_conv_w": CONV_W ** -0.5 * nrm(ks[18], (DEPTH, CONV_W, D_FF), jnp.float32),
        "ffn_conv_b": 0.02 * nrm(ks[19], (DEPTH, D_FF), jnp.float32),
        "ffn_w_down": D_FF ** -0.5 * nrm(ks[20], (DEPTH, D_FF, D), jnp.float32),
    }


def reference(x, c, ctx, c_ctx, norm1_g, norm2_g, ada_w, ada_b, even_w_in, even_w_s, even_b_s, even_w_out,
              odd_w_qkv, odd_q_g, odd_k_g, odd_rpb, odd_w_o, ffn_w_up, ffn_conv_w, ffn_conv_b, ffn_w_down):
    s_lat = jax.nn.silu(c)
    s_ctx = jax.nn.silu(c_ctx)
    for layer in range(DEPTH):
        last = layer == DEPTH - 1
        i = layer // 2
        mod = (s_lat @ ada_w[layer] + ada_b[layer])[:, None, :]
        mod_c = s_ctx @ ada_w[layer] + ada_b[layer]
        sh1, sc1, g1, sh2, sc2, g2 = jnp.split(mod, N_MOD, axis=-1)
        csh1, csc1, cg1, csh2, csc2, cg2 = jnp.split(mod_c, N_MOD, axis=-1)
        h = modulate(x, norm1_g[layer], sh1, sc1)
        h_c = modulate(ctx, norm1_g[layer], csh1, csc1)
        if layer % 2 == 0:
            y = even_mixer(h, even_w_in[i], even_w_s[i], even_b_s[i], even_w_out[i])
            if not last:
                y_c = even_mixer(h_c, even_w_in[i], even_w_s[i], even_b_s[i], even_w_out[i])
        else:
            q, k, v = qkv_heads(h, odd_w_qkv[i], odd_q_g[i], odd_k_g[i])
            q_c, k_c, v_c = qkv_heads(h_c, odd_w_qkv[i], odd_q_g[i], odd_k_g[i])
            y = _merge(neighbourhood_attention(q, k, v, k_c, v_c, odd_rpb[i])) @ odd_w_o[i]
            if not last:
                y_c = _merge(context_attention(q_c, k_c, v_c)) @ odd_w_o[i]
        x = x + g1 * y
        x = x + g2 * conv_ffn(modulate(x, norm2_g[layer], sh2, sc2),
                              ffn_w_up[layer], ffn_conv_w[layer], ffn_conv_b[layer], ffn_w_down[layer])
        if not last:
            ctx = ctx + cg1 * y_c
            ctx = ctx + cg2 * conv_ffn(modulate(ctx, norm2_g[layer], csh2, csc2),
                                       ffn_w_up[layer], ffn_conv_w[layer], ffn_conv_b[layer], ffn_w_down[layer])
    return x
```

```python
import functools

import numpy as np
import jax
import jax.numpy as jnp
from jax import lax
from jax.experimental import pallas as pl
from jax.experimental.pallas import tpu as pltpu

F32 = jnp.float32
BF16 = jnp.bfloat16

GRID_W = 64
WIN_H = 8
WIN_W = 16
CHUNK = 128
N_GROUPS = 4
GROUP_DIM = 128
HEAD_DIM = 64
N_MOD = 6
EPS = 1e-6
NEG_INF = -1e30

LANES = 128
MOD_ROWS = 16
HALO = 16
Q_ROWS = 4
VMEM_CAP = 60 * 1024 * 1024


def _vmem_limit(nbytes):
    return int(min(max(nbytes * 5 // 4, 32 * 1024 * 1024), VMEM_CAP))


def _params(nbytes, ngrid):
    return pltpu.CompilerParams(dimension_semantics=("parallel",) * ngrid,
                                vmem_limit_bytes=_vmem_limit(nbytes))


def _resident(shape):
    nd = len(shape)
    return pl.BlockSpec(shape, lambda *_: (0,) * nd, pipeline_mode=pl.Buffered(1))


def _rms_modulate(x, g, sh, sc):
    ms = jnp.mean(x * x, axis=-1, keepdims=True)
    return (x * lax.rsqrt(ms + EPS) * g) * (1.0 + sc) + sh


def _gelu_tanh(x):
    c = np.float32(np.sqrt(2.0 / np.pi))
    return x * (0.5 * (1.0 + jnp.tanh(c * (x + 0.044715 * (x * x * x)))))


def _dot(a, b):
    return jnp.dot(a, b, preferred_element_type=F32)


def _dot_t(a, b):
    return lax.dot_general(a, b, (((1,), (1,)), ((), ())), preferred_element_type=F32)


def _ada_kernel(c_ref, w_ref, b_ref, o_ref):
    s = c_ref[...]
    s = s * (0.5 * (1.0 + jnp.tanh(0.5 * s)))
    o_ref[0] = jnp.dot(s, w_ref[0], preferred_element_type=F32,
                       precision=lax.Precision.HIGHEST) + b_ref[0]


def _ada_mods(cond, ada_w, ada_b):
    depth, d, n = ada_w.shape
    tn = n // 4
    return pl.pallas_call(
        _ada_kernel,
        grid=(depth, n // tn),
        in_specs=[pl.BlockSpec((MOD_ROWS, d), lambda l, j: (0, 0)),
                  pl.BlockSpec((1, d, tn), lambda l, j: (l, 0, j)),
                  pl.BlockSpec((1, 1, tn), lambda l, j: (l, 0, j))],
        out_specs=pl.BlockSpec((1, MOD_ROWS, tn), lambda l, j: (l, 0, j)),
        out_shape=jax.ShapeDtypeStruct((depth, MOD_ROWS, n), F32),
        compiler_params=_params(2 * d * tn * 4 + 8 * d * tn * 2, 2),
        name="ada_mods",
    )(cond, ada_w, ada_b.reshape(depth, 1, n))


def _modmm_kernel(x_ref, g_ref, sh_ref, sc_ref, w_ref, o_ref, *, nc, gelu_from):
    h = _rms_modulate(x_ref[0], g_ref[...], sh_ref[0], sc_ref[0]).astype(BF16)
    n = w_ref.shape[1]
    for c0 in range(0, n, nc):
        a = _dot(h, w_ref[:, c0:c0 + nc])
        if c0 >= gelu_from:
            a = _gelu_tanh(a)
        o_ref[0, :, c0:c0 + nc] = a.astype(o_ref.dtype)


def _modmm(x, g, sh, sc, w, *, gelu_from):
    b, l, d = x.shape
    n = w.shape[1]
    tm = min(l, 512)
    nc = 512
    vec = pl.BlockSpec((1, 1, d), lambda i, j: (i, 0, 0))
    est = 2 * tm * d * 4 + d * n * 2 + 2 * tm * n * 2 + tm * d * 6 + tm * nc * 12
    return pl.pallas_call(
        functools.partial(_modmm_kernel, nc=nc, gelu_from=gelu_from),
        grid=(b, l // tm),
        in_specs=[pl.BlockSpec((1, tm, d), lambda i, j: (i, j, 0)),
                  _resident((1, d)), vec, vec, _resident((d, n))],
        out_specs=pl.BlockSpec((1, tm, n), lambda i, j: (i, j, 0)),
        out_shape=jax.ShapeDtypeStruct((b, l, n), BF16),
        compiler_params=_params(est, 2),
        name="modulate_proj",
    )(x, g, sh, sc, w)


def _fourier_kernel(x_ref, csc_ref, dft_ref, o_ref, r_ref, *, l, scale, tr):
    gd = GROUP_DIM
    for g in range(N_GROUPS):
        for r0 in range(0, l, tr):
            z = _dot(x_ref[0, r0:r0 + tr, g * gd:(g + 1) * gd], csc_ref[...])
            r_ref[r0:r0 + tr, g * gd:(g + 1) * gd] = z[:, :gd].astype(BF16)
            r_ref[l + r0:l + r0 + tr, g * gd:(g + 1) * gd] = z[:, gd:].astype(BF16)
    for r0 in range(0, l, tr):
        y = _dot(dft_ref[r0:r0 + tr, :], r_ref[...]) * scale
        o_ref[0, r0:r0 + tr, :] = y.astype(o_ref.dtype)


def _dft_tables(l):
    def cos_sin(n):
        k = lax.broadcasted_iota(jnp.int32, (n, n), 0)
        m = lax.broadcasted_iota(jnp.int32, (n, n), 1)
        ang = ((k * m) % n).astype(F32) * np.float32(2.0 * np.pi / n)
        return jnp.cos(ang), jnp.sin(ang)
    cl, sl = cos_sin(l)
    cc, sc = cos_sin(GROUP_DIM)
    return (jnp.concatenate([cl, sl], axis=1).astype(BF16),
            jnp.concatenate([cc, -sc], axis=1).astype(BF16))


def _fourier(p, dft, csc):
    b, l, _ = p.shape
    w = N_GROUPS * GROUP_DIM
    tr = min(l, 512)
    scale = float(1.0 / np.sqrt(l * GROUP_DIM))
    est = 2 * l * w * 2 * 2 + l * 2 * l * 2 + 2 * l * w * 2 + tr * w * 8
    return pl.pallas_call(
        functools.partial(_fourier_kernel, l=l, scale=scale, tr=tr),
        grid=(b,),
        in_specs=[pl.BlockSpec((1, l, w), lambda i: (i, 0, 0)),
                  _resident(csc.shape), _resident(dft.shape)],
        out_specs=pl.BlockSpec((1, l, w), lambda i: (i, 0, 0)),
        out_shape=jax.ShapeDtypeStruct((b, l, w), BF16),
        scratch_shapes=[pltpu.VMEM((2 * l, w), BF16)],
        compiler_params=_params(est, 1),
        name="fourier_mix",
    )(p, csc, dft)


def _sgu_outproj_kernel(a_ref, u_ref, v_ref, ws_ref, bs_ref, w_ref, x_ref, gate_ref, o_ref, b_scr):
    tm = a_ref.shape[1]
    gd = GROUP_DIM
    for g in range(N_GROUPS):
        v = v_ref[0, :, g * gd:(g + 1) * gd].astype(F32)
        vn = (v * lax.rsqrt(jnp.mean(v * v, axis=-1, keepdims=True) + EPS)).astype(BF16)
        for r0 in range(0, tm, CHUNK):
            s = _dot(ws_ref[g], vn[r0:r0 + CHUNK]) + bs_ref[g]
            u = u_ref[0, r0:r0 + CHUNK, g * gd:(g + 1) * gd].astype(F32)
            b_scr[r0:r0 + CHUNK, g * gd:(g + 1) * gd] = (u * s).astype(BF16)
    wa = a_ref.shape[2]
    y = _dot(a_ref[0], w_ref[0:wa, :]) + _dot(b_scr[...], w_ref[wa:, :])
    o_ref[0] = x_ref[0] + gate_ref[0] * y


def _sgu_outproj(a, p, w_s, b_s, w_out, x, gate):
    b, l, d = x.shape
    wa = a.shape[2]
    wb = N_GROUPS * GROUP_DIM
    tm = min(l, 512)
    est = 2 * tm * (wa + 2 * wb) * 2 + d * d * 2 + 4 * tm * d * 4 + tm * wb * 2 + tm * d * 8
    return pl.pallas_call(
        _sgu_outproj_kernel,
        grid=(b, l // tm),
        in_specs=[pl.BlockSpec((1, tm, wa), lambda i, j: (i, j, 0)),
                  pl.BlockSpec((1, tm, wb), lambda i, j: (i, j, wa // wb)),
                  pl.BlockSpec((1, tm, wb), lambda i, j: (i, j, wa // wb + 1)),
                  _resident(w_s.shape), _resident(b_s.shape), _resident(w_out.shape),
                  pl.BlockSpec((1, tm, d), lambda i, j: (i, j, 0)),
                  pl.BlockSpec((1, 1, d), lambda i, j: (i, 0, 0))],
        out_specs=pl.BlockSpec((1, tm, d), lambda i, j: (i, j, 0)),
        out_shape=jax.ShapeDtypeStruct((b, l, d), F32),
        scratch_shapes=[pltpu.VMEM((tm, wb), BF16)],
        compiler_params=_params(est, 2),
        name="sgu_out_proj",
    )(a, p, p, w_s, b_s, w_out, x, gate)


def _outproj_kernel(a_ref, w_ref, x_ref, gate_ref, o_ref):
    o_ref[0] = x_ref[0] + gate_ref[0] * _dot(a_ref[0], w_ref[...])


def _outproj(a, w_out, x, gate):
    b, l, d = x.shape
    k = a.shape[2]
    tm = min(l, 512)
    est = 2 * tm * k * 2 + k * d * 2 + 4 * tm * d * 4 + tm * d * 8
    return pl.pallas_call(
        _outproj_kernel,
        grid=(b, l // tm),
        in_specs=[pl.BlockSpec((1, tm, k), lambda i, j: (i, j, 0)),
                  _resident(w_out.shape),
                  pl.BlockSpec((1, tm, d), lambda i, j: (i, j, 0)),
                  pl.BlockSpec((1, 1, d), lambda i, j: (i, 0, 0))],
        out_specs=pl.BlockSpec((1, tm, d), lambda i, j: (i, j, 0)),
        out_shape=jax.ShapeDtypeStruct((b, l, d), F32),
        compiler_params=_params(est, 2),
        name="out_proj",
    )(a, w_out, x, gate)


def _pair_norm(x, gain, head0):
    sq = x * x
    s0 = jnp.sum(jnp.where(head0, sq, 0.0), axis=-1, keepdims=True)
    s1 = jnp.sum(jnp.where(head0, 0.0, sq), axis=-1, keepdims=True)
    inv = 1.0 / HEAD_DIM
    rinv = jnp.where(head0, lax.rsqrt(s0 * inv + EPS), lax.rsqrt(s1 * inv + EPS))
    return x * rinv * gain


def _softmax_pv(scores, values):
    m = scores[0].max(axis=-1, keepdims=True)
    for s in scores[1:]:
        m = jnp.maximum(m, s.max(axis=-1, keepdims=True))
    ps = [jnp.exp(s - m) for s in scores]
    den = ps[0].sum(axis=-1, keepdims=True)
    for p in ps[1:]:
        den = den + p.sum(axis=-1, keepdims=True)
    o = _dot(ps[0].astype(BF16), values[0])
    for p, v in zip(ps[1:], values[1:]):
        o = o + _dot(p.astype(BF16), v)
    return o / den


def _nbr_attn_kernel(q_ref, k_ref, v_ref, kc_ref, vc_ref, qg_ref, kg_ref, bias_ref, o_ref,
                     kn_ref, kcn_ref, *, blocks, scale):
    head0 = lax.broadcasted_iota(jnp.int32, (1, LANES), 1) < HEAD_DIM
    l = k_ref.shape[1]
    tr = 512
    for r0 in range(0, l, tr):
        kn_ref[r0:r0 + tr, :] = _pair_norm(k_ref[0, r0:r0 + tr, :].astype(F32), kg_ref[...], head0).astype(BF16)
    kcn_ref[...] = _pair_norm(kc_ref[0].astype(F32), kg_ref[...], head0).astype(BF16)
    nq = Q_ROWS * GRID_W
    for j, (k0, nk, t) in enumerate(blocks):
        qn = _pair_norm(q_ref[0, j * nq:(j + 1) * nq, :].astype(F32), qg_ref[...], head0) * scale
        outs = []
        for h in range(2):
            sel = head0 if h == 0 else jnp.logical_not(head0)
            qh = jnp.where(sel, qn, 0.0).astype(BF16)
            s_loc = _dot_t(qh, kn_ref[k0:k0 + nk, :]) + bias_ref[h, t]
            s_ctx = _dot_t(qh, kcn_ref[...])
            outs.append(_softmax_pv([s_loc, s_ctx], [v_ref[0, k0:k0 + nk, :], vc_ref[0]]))
        o_ref[0, j * nq:(j + 1) * nq, :] = jnp.where(head0, outs[0], outs[1]).astype(o_ref.dtype)


def _nbr_blocks(rows):
    kh = min(WIN_H, rows)
    kr = min(Q_ROWS + kh, rows)
    plans, types = [], []
    for j in range(rows // Q_ROWS):
        ks = int(np.clip(Q_ROWS * j - kh // 2, 0, rows - kr))
        d = np.full((Q_ROWS, kr), -1, np.int64)
        for rq in range(Q_ROWS):
            r = Q_ROWS * j + rq
            rs = int(np.clip(r - kh // 2, 0, rows - kh))
            for i in range(kh):
                d[rq, rs + i - ks] = rs + i - r + (WIN_H - 1)
        key = d.tobytes()
        if key not in [t.tobytes() for t in types]:
            types.append(d)
        t = [tt.tobytes() for tt in types].index(key)
        plans.append((ks * GRID_W, kr * GRID_W, t))
    return tuple(plans), np.stack(types)


def _nbr_bias(rpb, dtypes):
    col = np.arange(GRID_W)
    cs = np.clip(col - WIN_W // 2, 0, GRID_W - WIN_W)
    cmask = (col[None, :] >= cs[:, None]) & (col[None, :] < cs[:, None] + WIN_W)
    dc = np.clip(col[None, :] - col[:, None] + (WIN_W - 1), 0, 2 * WIN_W - 2)
    valid = (dtypes >= 0)[:, :, :, None, None] & cmask[None, None, None]
    b = rpb[:, np.maximum(dtypes, 0)][..., dc]
    b = jnp.where(valid[None], b, NEG_INF).transpose(0, 1, 2, 4, 3, 5)
    h, t, rq, qc, kr, kc = b.shape
    return b.reshape(h, t, rq * qc, kr * kc)


def _nbr_attention(qkv, qkv_c, q_g, k_g, bias, blocks):
    b, l, d3 = qkv.shape
    d = d3 // 3
    lc = qkv_c.shape[1]
    npair = d // LANES
    nt = bias.shape[1]
    nq, nk = bias.shape[2], bias.shape[3]
    est = (2 * 4 * l * LANES * 2 + 2 * 2 * lc * LANES * 2 + 2 * 2 * nt * nq * nk * 4
           + (l + lc) * LANES * 2 + 6 * nq * (nk + lc) * 4)
    return pl.pallas_call(
        functools.partial(_nbr_attn_kernel, blocks=blocks, scale=float(HEAD_DIM ** -0.5)),
        grid=(npair, b),
        in_specs=[pl.BlockSpec((1, l, LANES), lambda p, i: (i, 0, p)),
                  pl.BlockSpec((1, l, LANES), lambda p, i: (i, 0, npair + p)),
                  pl.BlockSpec((1, l, LANES), lambda p, i: (i, 0, 2 * npair + p)),
                  pl.BlockSpec((1, lc, LANES), lambda p, i: (i, 0, npair + p)),
                  pl.BlockSpec((1, lc, LANES), lambda p, i: (i, 0, 2 * npair + p)),
                  _resident((1, LANES)), _resident((1, LANES)),
                  pl.BlockSpec((2, nt, nq, nk), lambda p, i: (p, 0, 0, 0))],
        out_specs=pl.BlockSpec((1, l, LANES), lambda p, i: (i, 0, p)),
        out_shape=jax.ShapeDtypeStruct((b, l, d), BF16),
        scratch_shapes=[pltpu.VMEM((l, LANES), BF16), pltpu.VMEM((lc, LANES), BF16)],
        compiler_params=_params(est, 2),
        name="nbr_attention",
    )(qkv, qkv, qkv, qkv_c, qkv_c, q_g, k_g, bias)


def _ctx_attn_kernel(q_ref, k_ref, v_ref, qg_ref, kg_ref, o_ref, *, scale):
    head0 = lax.broadcasted_iota(jnp.int32, (1, LANES), 1) < HEAD_DIM
    kn = _pair_norm(k_ref[0].astype(F32), kg_ref[...], head0).astype(BF16)
    qn = _pair_norm(q_ref[0].astype(F32), qg_ref[...], head0) * scale
    outs = []
    for h in range(2):
        sel = head0 if h == 0 else jnp.logical_not(head0)
        qh = jnp.where(sel, qn, 0.0).astype(BF16)
        outs.append(_softmax_pv([_dot_t(qh, kn)], [v_ref[0]]))
    o_ref[0] = jnp.where(head0, outs[0], outs[1]).astype(o_ref.dtype)


def _ctx_attention(qkv_c, q_g, k_g):
    b, lc, d3 = qkv_c.shape
    d = d3 // 3
    npair = d // LANES
    est = 2 * 4 * lc * LANES * 2 + 8 * lc * lc * 4
    return pl.pallas_call(
        functools.partial(_ctx_attn_kernel, scale=float(HEAD_DIM ** -0.5)),
        grid=(b, npair),
        in_specs=[pl.BlockSpec((1, lc, LANES), lambda i, p: (i, 0, p)),
                  pl.BlockSpec((1, lc, LANES), lambda i, p: (i, 0, npair + p)),
                  pl.BlockSpec((1, lc, LANES), lambda i, p: (i, 0, 2 * npair + p)),
                  _resident((1, LANES)), _resident((1, LANES))],
        out_specs=pl.BlockSpec((1, lc, LANES), lambda i, p: (i, 0, p)),
        out_shape=jax.ShapeDtypeStruct((b, lc, d), BF16),
        compiler_params=_params(est, 2),
        name="ctx_attention",
    )(qkv_c, qkv_c, qkv_c, q_g, k_g)


def _ffn_kernel(x_ref, xp_ref, xn_ref, g_ref, sh_ref, sc_ref, gate_ref, wu_ref, cw_ref, cb_ref, wd_ref,
                o_ref, h_ref, acc_ref, *, chunks):
    j = pl.program_id(1)
    tm = x_ref.shape[1]
    dff = wd_ref.shape[0]
    g, sh, sc = g_ref[...], sh_ref[0], sc_ref[0]
    keep_prev = (j > 0).astype(F32)
    keep_next = (j < pl.num_programs(1) - 1).astype(F32)
    h_ref[0:HALO, :] = (_rms_modulate(xp_ref[0], g, sh, sc) * keep_prev).astype(BF16)
    h_ref[HALO:HALO + tm, :] = _rms_modulate(x_ref[0], g, sh, sc).astype(BF16)
    h_ref[HALO + tm:, :] = (_rms_modulate(xn_ref[0], g, sh, sc) * keep_next).astype(BF16)
    ext = tm + 2 * HALO
    for c0, nc in chunks:
        ge = _dot(h_ref[...], wu_ref[:, c0:c0 + nc])
        val = _dot(h_ref[HALO:HALO + tm, :], wu_ref[:, dff + c0:dff + c0 + nc])
        gp = pltpu.roll(ge, 1, axis=0)[HALO:HALO + tm]
        gn = pltpu.roll(ge, ext - 1, axis=0)[HALO:HALO + tm]
        gc = ge[HALO:HALO + tm]
        cw = cw_ref[:, c0:c0 + nc]
        t = gp * cw[0:1] + gc * cw[1:2] + gn * cw[2:3] + cb_ref[:, c0:c0 + nc]
        ht = 0.5 * t
        act = ((ht + ht * jnp.tanh(ht)) * val).astype(BF16)
        part = _dot(act, wd_ref[c0:c0 + nc, :])
        if c0 == 0:
            acc_ref[...] = part
        else:
            acc_ref[...] += part
    o_ref[0] = x_ref[0] + gate_ref[0] * acc_ref[...]


def _ffn(x, g, sh, sc, gate, w_up, conv_w, conv_b, w_down):
    b, l, d = x.shape
    dff = w_down.shape[0]
    tm = min(l, 512)
    nt = l // tm
    hb = tm // HALO
    nhb = l // HALO
    chunks, c0 = [], 0
    while c0 < dff:
        nc = min(512, dff - c0)
        chunks.append((c0, nc))
        c0 += nc
    vec = pl.BlockSpec((1, 1, d), lambda i, j: (i, 0, 0))
    est = (4 * tm * d * 4 + 4 * HALO * d * 4 + 3 * d * dff * 2 + (tm + 2 * HALO) * d * 2 + tm * d * 4
           + 8 * (tm + 2 * HALO) * 512 * 4)
    return pl.pallas_call(
        functools.partial(_ffn_kernel, chunks=tuple(chunks)),
        grid=(b, nt),
        in_specs=[pl.BlockSpec((1, tm, d), lambda i, j: (i, j, 0)),
                  pl.BlockSpec((1, HALO, d), lambda i, j: (i, jnp.maximum(j * hb - 1, 0), 0)),
                  pl.BlockSpec((1, HALO, d), lambda i, j: (i, jnp.minimum((j + 1) * hb, nhb - 1), 0)),
                  _resident((1, d)), vec, vec, vec,
                  _resident(w_up.shape), _resident(conv_w.shape), _resident((1, dff)), _resident(w_down.shape)],
        out_specs=pl.BlockSpec((1, tm, d), lambda i, j: (i, j, 0)),
        out_shape=jax.ShapeDtypeStruct((b, l, d), F32),
        scratch_shapes=[pltpu.VMEM((tm + 2 * HALO, d), BF16), pltpu.VMEM((tm, d), F32)],
        compiler_params=_params(est, 2),
        name="conv_ffn",
    )(x, x, x, g, sh, sc, gate, w_up, conv_w, conv_b.reshape(1, dff), w_down)


def kernel(x, c, ctx, c_ctx, norm1_g, norm2_g, ada_w, ada_b, even_w_in, even_w_s, even_b_s, even_w_out,
           odd_w_qkv, odd_q_g, odd_k_g, odd_rpb, odd_w_o, ffn_w_up, ffn_conv_w, ffn_conv_b, ffn_w_down):
    b, l, d = x.shape
    lc = ctx.shape[1]
    depth = ada_w.shape[0]
    assert b + 1 <= MOD_ROWS and l % (Q_ROWS * GRID_W) == 0

    cond = jnp.zeros((MOD_ROWS, d), F32).at[:b].set(c).at[b].set(c_ctx)
    mods = _ada_mods(cond, ada_w, ada_b).reshape(depth, MOD_ROWS, N_MOD, d)

    bf = lambda w: w.astype(BF16)
    w_in, w_s, w_out = bf(even_w_in), bf(even_w_s), bf(even_w_out)
    w_qkv, w_o = bf(odd_w_qkv), bf(odd_w_o)
    w_up, w_down = bf(ffn_w_up), bf(ffn_w_down)
    b_s = jnp.broadcast_to(even_b_s[..., None], even_b_s.shape + (GROUP_DIM,))
    fw = N_GROUPS * GROUP_DIM

    dft_x, csc = _dft_tables(l)
    dft_c, _ = _dft_tables(lc)
    blocks, bias_types = _nbr_blocks(l // GRID_W)
    pair = lambda v: jnp.tile(v, LANES // HEAD_DIM).reshape(1, LANES)

    for layer in range(depth):
        last = layer == depth - 1
        i = layer // 2
        m_lat = [mods[layer, :b, k][:, None, :] for k in range(N_MOD)]
        m_ctx = [jnp.broadcast_to(mods[layer, b, k][None, None, :], (b, 1, d)) for k in range(N_MOD)]
        n1, n2 = norm1_g[layer][None, :], norm2_g[layer][None, :]

        if layer % 2 == 0:
            p = _modmm(x, n1, m_lat[0], m_lat[1], w_in[i], gelu_from=fw)
            x = _sgu_outproj(_fourier(p, dft_x, csc), p, w_s[i], b_s[i], w_out[i], x, m_lat[2])
            if not last:
                p_c = _modmm(ctx, n1, m_ctx[0], m_ctx[1], w_in[i], gelu_from=fw)
                ctx = _sgu_outproj(_fourier(p_c, dft_c, csc), p_c, w_s[i], b_s[i], w_out[i], ctx, m_ctx[2])
        else:
            qkv = _modmm(x, n1, m_lat[0], m_lat[1], w_qkv[i], gelu_from=3 * d)
            qkv_c = _modmm(ctx, n1, m_ctx[0], m_ctx[1], w_qkv[i], gelu_from=3 * d)
            q_g, k_g = pair(odd_q_g[i]), pair(odd_k_g[i])
            o = _nbr_attention(qkv, qkv_c, q_g, k_g, _nbr_bias(odd_rpb[i], bias_types), blocks)
            x = _outproj(o, w_o[i], x, m_lat[2])
            if not last:
                ctx = _outproj(_ctx_attention(qkv_c, q_g, k_g), w_o[i], ctx, m_ctx[2])

        x = _ffn(x, n2, m_lat[3], m_lat[4], m_lat[5], w_up[layer], ffn_conv_w[layer], ffn_conv_b[layer],
                 w_down[layer])
        if not last:
            ctx = _ffn(ctx, n2, m_ctx[3], m_ctx[4], m_ctx[5], w_up[layer], ffn_conv_w[layer],
                       ffn_conv_b[layer], w_down[layer])
    return x
```

```python
import functools

import numpy as np
import jax
import jax.numpy as jnp
from jax import lax
from jax.experimental import pallas as pl
from jax.experimental.pallas import tpu as pltpu

F32 = jnp.float32
BF16 = jnp.bfloat16

GRID_W = 64
WIN_H = 8
WIN_W = 16
CHUNK = 128
N_GROUPS = 4
GROUP_DIM = 128
HEAD_DIM = 64
N_MOD = 6
EPS = 1e-6
NEG_INF = -1e30

LANES = 128
MOD_ROWS = 16
HALO = 16
Q_ROWS = 4
DFT_ROWS = 256
VMEM_CAP = 60 * 1024 * 1024


def _vmem_limit(nbytes):
    return int(min(max(nbytes * 5 // 4, 32 * 1024 * 1024), VMEM_CAP))


def _params(nbytes, semantics):
    return pltpu.CompilerParams(dimension_semantics=semantics, vmem_limit_bytes=_vmem_limit(nbytes))


def _resident(shape):
    nd = len(shape)
    return pl.BlockSpec(shape, lambda *_: (0,) * nd, pipeline_mode=pl.Buffered(1))


def _layer_spec(arr, idx):
    nd = arr.ndim
    return pl.BlockSpec((1,) + arr.shape[1:], lambda *_: (idx,) + (0,) * (nd - 1), pipeline_mode=pl.Buffered(1))


def _mod_spec(mods, layer, k, row):
    blk = (1, 1, 1, 1, mods.shape[-1])
    if row is None:
        return pl.BlockSpec(blk, lambda i, *_: (layer, k, i, 0, 0))
    return pl.BlockSpec(blk, lambda *_: (layer, k, row, 0, 0))


def _rms_modulate(x, g, sh, sc):
    ms = jnp.mean(x * x, axis=-1, keepdims=True)
    return (x * lax.rsqrt(ms + EPS) * g) * (1.0 + sc) + sh


def _gelu_tanh(x):
    c = np.float32(np.sqrt(2.0 / np.pi))
    return x * (0.5 * (1.0 + jnp.tanh(c * (x + 0.044715 * (x * x * x)))))


def _dot(a, b):
    return jnp.dot(a, b, preferred_element_type=F32)


def _dot_t(a, b):
    return lax.dot_general(a, b, (((1,), (1,)), ((), ())), preferred_element_type=F32)


def _ada_kernel(c_ref, w_ref, b_ref, o_ref):
    s = c_ref[...]
    s = s * (0.5 * (1.0 + jnp.tanh(0.5 * s)))
    o_ref[0, 0] = jnp.dot(s, w_ref[0], preferred_element_type=F32,
                          precision=lax.Precision.HIGHEST) + b_ref[0, 0]


def _ada_mods(cond, ada_w, ada_b):
    depth, d, n = ada_w.shape
    nm = n // d
    return pl.pallas_call(
        _ada_kernel,
        grid=(depth, nm),
        in_specs=[pl.BlockSpec((MOD_ROWS, d), lambda l, k: (0, 0)),
                  pl.BlockSpec((1, d, d), lambda l, k: (l, 0, k)),
                  pl.BlockSpec((1, 1, 1, d), lambda l, k: (l, k, 0, 0))],
        out_specs=pl.BlockSpec((1, 1, MOD_ROWS, d), lambda l, k: (l, k, 0, 0)),
        out_shape=jax.ShapeDtypeStruct((depth, nm, MOD_ROWS, d), F32),
        compiler_params=_params(2 * d * d * 4 + 8 * d * d * 2, ("parallel", "parallel")),
        name="ada_mods",
    )(cond, ada_w, ada_b.reshape(depth, nm, 1, d))


def _modmm_kernel(x_ref, g_ref, sh_ref, sc_ref, w_ref, o_ref, *, nc, gelu_from):
    h = _rms_modulate(x_ref[0], g_ref[0], sh_ref[0, 0, 0], sc_ref[0, 0, 0]).astype(BF16)
    n = w_ref.shape[2]
    for c0 in range(0, n, nc):
        a = _dot(h, w_ref[0, :, c0:c0 + nc])
        if c0 >= gelu_from:
            a = _gelu_tanh(a)
        o_ref[0, :, c0:c0 + nc] = a.astype(o_ref.dtype)


def _modmm(x, norm_g, mods, layer, row, w, widx, *, gelu_from):
    b, l, d = x.shape
    n = w.shape[2]
    tm = min(l, 512)
    nc = 512
    est = 2 * tm * d * 4 + d * n * 2 + 2 * tm * n * 2 + tm * d * 6 + tm * nc * 12
    return pl.pallas_call(
        functools.partial(_modmm_kernel, nc=nc, gelu_from=gelu_from),
        grid=(b, l // tm),
        in_specs=[pl.BlockSpec((1, tm, d), lambda i, j: (i, j, 0)),
                  _layer_spec(norm_g, layer), _mod_spec(mods, layer, 0, row), _mod_spec(mods, layer, 1, row),
                  _layer_spec(w, widx)],
        out_specs=pl.BlockSpec((1, tm, n), lambda i, j: (i, j, 0)),
        out_shape=jax.ShapeDtypeStruct((b, l, n), BF16),
        compiler_params=_params(est, ("parallel", "parallel")),
        name="modulate_proj",
    )(x, norm_g, mods, mods, w)


def _fourier_kernel(x_ref, csc_ref, rc_ref, rs_ref, kc_ref, ks_ref, o_ref, dft_ref, r_ref, *, l, scale, tr):
    @pl.when(pl.program_id(0) == 0)
    def _():
        nr = rc_ref.shape[0]
        for i in range(l // nr):
            kc, ks = kc_ref[i], ks_ref[i]
            dft_ref[i * nr:(i + 1) * nr, 0:l] = (kc * rc_ref[...] - ks * rs_ref[...]).astype(BF16)
            dft_ref[i * nr:(i + 1) * nr, l:2 * l] = (ks * rc_ref[...] + kc * rs_ref[...]).astype(BF16)

    gd = GROUP_DIM
    for g in range(N_GROUPS):
        for r0 in range(0, l, tr):
            z = _dot(x_ref[0, r0:r0 + tr, g * gd:(g + 1) * gd], csc_ref[...])
            r_ref[r0:r0 + tr, g * gd:(g + 1) * gd] = z[:, :gd].astype(BF16)
            r_ref[l + r0:l + r0 + tr, g * gd:(g + 1) * gd] = z[:, gd:].astype(BF16)
    for r0 in range(0, l, tr):
        y = _dot(dft_ref[r0:r0 + tr, :], r_ref[...]) * scale
        o_ref[0, r0:r0 + tr, :] = y.astype(o_ref.dtype)


def _cos_sin(rows, n):
    ang = ((rows.astype(np.int64)[:, None] * np.arange(n, dtype=np.int64)[None, :]) % n) * (2.0 * np.pi / n)
    return np.cos(ang).astype(np.float32), np.sin(ang).astype(np.float32)


def _fourier(p, csc):
    b, l, _ = p.shape
    w = N_GROUPS * GROUP_DIM
    tr = min(l, 512)
    nr = min(l, DFT_ROWS)
    rc, rs = _cos_sin(np.arange(nr), l)
    kc, ks = _cos_sin(np.arange(0, l, nr), l)
    kc, ks = kc[:, None, :], ks[:, None, :]
    scale = float(1.0 / np.sqrt(l * GROUP_DIM))
    est = 2 * l * w * 2 * 2 + l * 2 * l * 2 + 2 * l * w * 2 + tr * w * 8 + 2 * nr * l * 4 + 4 * nr * l * 4
    return pl.pallas_call(
        functools.partial(_fourier_kernel, l=l, scale=scale, tr=tr),
        grid=(b,),
        in_specs=[pl.BlockSpec((1, l, w), lambda i: (i, 0, 0)),
                  _resident(csc.shape), _resident(rc.shape), _resident(rs.shape),
                  _resident(kc.shape), _resident(ks.shape)],
        out_specs=pl.BlockSpec((1, l, w), lambda i: (i, 0, 0)),
        out_shape=jax.ShapeDtypeStruct((b, l, w), BF16),
        scratch_shapes=[pltpu.VMEM((l, 2 * l), BF16), pltpu.VMEM((2 * l, w), BF16)],
        compiler_params=_params(est, ("arbitrary",)),
        name="fourier_mix",
    )(p, csc, rc, rs, kc, ks)


def _sgu_outproj_kernel(a_ref, u_ref, v_ref, ws_ref, bs_ref, w_ref, x_ref, gate_ref, o_ref, b_scr):
    tm = a_ref.shape[1]
    gd = GROUP_DIM
    for g in range(N_GROUPS):
        v = v_ref[0, :, g * gd:(g + 1) * gd].astype(F32)
        vn = (v * lax.rsqrt(jnp.mean(v * v, axis=-1, keepdims=True) + EPS)).astype(BF16)
        for r0 in range(0, tm, CHUNK):
            s = _dot(ws_ref[0, g], vn[r0:r0 + CHUNK]) + bs_ref[0, g]
            u = u_ref[0, r0:r0 + CHUNK, g * gd:(g + 1) * gd].astype(F32)
            b_scr[r0:r0 + CHUNK, g * gd:(g + 1) * gd] = (u * s).astype(BF16)
    wa = a_ref.shape[2]
    y = _dot(a_ref[0], w_ref[0, 0:wa, :]) + _dot(b_scr[...], w_ref[0, wa:, :])
    o_ref[0] = x_ref[0] + gate_ref[0, 0, 0] * y


def _sgu_outproj(a, p, w_s, b_s, w_out, widx, x, mods, layer, row):
    b, l, d = x.shape
    wa = a.shape[2]
    wb = N_GROUPS * GROUP_DIM
    tm = min(l, 512)
    est = 2 * tm * (wa + 2 * wb) * 2 + d * d * 2 + 4 * tm * d * 4 + tm * wb * 2 + tm * d * 8
    return pl.pallas_call(
        _sgu_outproj_kernel,
        grid=(b, l // tm),
        in_specs=[pl.BlockSpec((1, tm, wa), lambda i, j: (i, j, 0)),
                  pl.BlockSpec((1, tm, wb), lambda i, j: (i, j, wa // wb)),
                  pl.BlockSpec((1, tm, wb), lambda i, j: (i, j, wa // wb + 1)),
                  _layer_spec(w_s, widx), _layer_spec(b_s, widx), _layer_spec(w_out, widx),
                  pl.BlockSpec((1, tm, d), lambda i, j: (i, j, 0)),
                  _mod_spec(mods, layer, 2, row)],
        out_specs=pl.BlockSpec((1, tm, d), lambda i, j: (i, j, 0)),
        out_shape=jax.ShapeDtypeStruct((b, l, d), F32),
        scratch_shapes=[pltpu.VMEM((tm, wb), BF16)],
        compiler_params=_params(est, ("parallel", "parallel")),
        name="sgu_out_proj",
    )(a, p, p, w_s, b_s, w_out, x, mods)


def _outproj_kernel(a_ref, w_ref, x_ref, gate_ref, o_ref):
    o_ref[0] = x_ref[0] + gate_ref[0, 0, 0] * _dot(a_ref[0], w_ref[0])


def _outproj(a, w_out, widx, x, mods, layer, row):
    b, l, d = x.shape
    k = a.shape[2]
    tm = min(l, 512)
    est = 2 * tm * k * 2 + k * d * 2 + 4 * tm * d * 4 + tm * d * 8
    return pl.pallas_call(
        _outproj_kernel,
        grid=(b, l // tm),
        in_specs=[pl.BlockSpec((1, tm, k), lambda i, j: (i, j, 0)),
                  _layer_spec(w_out, widx),
                  pl.BlockSpec((1, tm, d), lambda i, j: (i, j, 0)),
                  _mod_spec(mods, layer, 2, row)],
        out_specs=pl.BlockSpec((1, tm, d), lambda i, j: (i, j, 0)),
        out_shape=jax.ShapeDtypeStruct((b, l, d), F32),
        compiler_params=_params(est, ("parallel", "parallel")),
        name="out_proj",
    )(a, w_out, x, mods)


def _pair_norm(x, gain, head0):
    sq = x * x
    s0 = jnp.sum(jnp.where(head0, sq, 0.0), axis=-1, keepdims=True)
    s1 = jnp.sum(jnp.where(head0, 0.0, sq), axis=-1, keepdims=True)
    inv = 1.0 / HEAD_DIM
    rinv = jnp.where(head0, lax.rsqrt(s0 * inv + EPS), lax.rsqrt(s1 * inv + EPS))
    return x * rinv * gain


def _softmax_pv(scores, values):
    m = scores[0].max(axis=-1, keepdims=True)
    for s in scores[1:]:
        m = jnp.maximum(m, s.max(axis=-1, keepdims=True))
    ps = [jnp.exp(s - m) for s in scores]
    den = ps[0].sum(axis=-1, keepdims=True)
    for p in ps[1:]:
        den = den + p.sum(axis=-1, keepdims=True)
    o = _dot(ps[0].astype(BF16), values[0])
    for p, v in zip(ps[1:], values[1:]):
        o = o + _dot(p.astype(BF16), v)
    return o / den


def _nbr_attn_kernel(q_ref, k_ref, v_ref, kc_ref, vc_ref, qg_ref, kg_ref, m_ref, o_ref,
                     kn_ref, kcn_ref, bias_ref, *, blocks, dplan, scale):
    head0 = lax.broadcasted_iota(jnp.int32, (1, LANES), 1) < HEAD_DIM
    w = GRID_W

    @pl.when(pl.program_id(1) == 0)
    def _():
        for h in range(2):
            for t, rows in enumerate(dplan):
                for rq, ds in enumerate(rows):
                    for kp in range(len(ds) // 2):
                        piece = jnp.concatenate([m_ref[0, h, ds[2 * kp]], m_ref[0, h, ds[2 * kp + 1]]], axis=-1)
                        bias_ref[h, t, rq * w:(rq + 1) * w, 2 * kp * w:(2 * kp + 2) * w] = piece

    l = k_ref.shape[1]
    tr = 512
    for r0 in range(0, l, tr):
        kn_ref[r0:r0 + tr, :] = _pair_norm(k_ref[0, r0:r0 + tr, :].astype(F32), kg_ref[0], head0).astype(BF16)
    kcn_ref[...] = _pair_norm(kc_ref[0].astype(F32), kg_ref[0], head0).astype(BF16)
    nq = Q_ROWS * w
    for j, (k0, nk, t) in enumerate(blocks):
        qn = _pair_norm(q_ref[0, j * nq:(j + 1) * nq, :].astype(F32), qg_ref[0], head0) * scale
        outs = []
        for h in range(2):
            sel = head0 if h == 0 else jnp.logical_not(head0)
            qh = jnp.where(sel, qn, 0.0).astype(BF16)
            s_loc = _dot_t(qh, kn_ref[k0:k0 + nk, :]) + bias_ref[h, t]
            s_ctx = _dot_t(qh, kcn_ref[...])
            outs.append(_softmax_pv([s_loc, s_ctx], [v_ref[0, k0:k0 + nk, :], vc_ref[0]]))
        o_ref[0, j * nq:(j + 1) * nq, :] = jnp.where(head0, outs[0], outs[1]).astype(o_ref.dtype)


def _nbr_plan(rows):
    kh = min(WIN_H, rows)
    kr = min(Q_ROWS + kh, rows)
    assert kr % 2 == 0 and rows % Q_ROWS == 0
    outside = 2 * WIN_H - 1
    blocks, types = [], []
    for j in range(rows // Q_ROWS):
        ks = int(np.clip(Q_ROWS * j - kh // 2, 0, rows - kr))
        d = [[outside] * kr for _ in range(Q_ROWS)]
        for rq in range(Q_ROWS):
            r = Q_ROWS * j + rq
            rs = int(np.clip(r - kh // 2, 0, rows - kh))
            for i in range(kh):
                d[rq][rs + i - ks] = rs + i - r + (WIN_H - 1)
        d = tuple(tuple(row) for row in d)
        if d not in types:
            types.append(d)
        blocks.append((ks * GRID_W, kr * GRID_W, types.index(d)))
    return tuple(blocks), tuple(types)


def _bias_pieces(rpb):
    col = np.arange(GRID_W)
    cs = np.clip(col - WIN_W // 2, 0, GRID_W - WIN_W)
    cmask = (col[None, :] >= cs[:, None]) & (col[None, :] < cs[:, None] + WIN_W)
    dc = np.clip(col[None, :] - col[:, None] + (WIN_W - 1), 0, 2 * WIN_W - 2)
    m = jnp.where(cmask, rpb[..., dc], NEG_INF)
    return jnp.concatenate([m, jnp.full_like(m[:, :, :1], NEG_INF)], axis=2)


def _nbr_attention(qkv, qkv_c, q_g, k_g, pieces, idx):
    b, l, d3 = qkv.shape
    d = d3 // 3
    lc = qkv_c.shape[1]
    npair = d // LANES
    blocks, dplan = _nbr_plan(l // GRID_W)
    nt = len(dplan)
    nq, nk = Q_ROWS * GRID_W, blocks[0][1]
    est = (2 * 4 * l * LANES * 2 + 2 * 2 * lc * LANES * 2 + 2 * 2 * pieces.shape[2] * GRID_W * LANES * 4
           + 2 * nt * nq * nk * 4 + (l + lc) * LANES * 2 + 6 * nq * (nk + lc) * 4)
    return pl.pallas_call(
        functools.partial(_nbr_attn_kernel, blocks=blocks, dplan=dplan, scale=float(HEAD_DIM ** -0.5)),
        grid=(npair, b),
        in_specs=[pl.BlockSpec((1, l, LANES), lambda p, i: (i, 0, p)),
                  pl.BlockSpec((1, l, LANES), lambda p, i: (i, 0, npair + p)),
                  pl.BlockSpec((1, l, LANES), lambda p, i: (i, 0, 2 * npair + p)),
                  pl.BlockSpec((1, lc, LANES), lambda p, i: (i, 0, npair + p)),
                  pl.BlockSpec((1, lc, LANES), lambda p, i: (i, 0, 2 * npair + p)),
                  _layer_spec(q_g, idx), _layer_spec(k_g, idx),
                  pl.BlockSpec((1, 2) + pieces.shape[2:], lambda p, i: (idx, p, 0, 0, 0))],
        out_specs=pl.BlockSpec((1, l, LANES), lambda p, i: (i, 0, p)),
        out_shape=jax.ShapeDtypeStruct((b, l, d), BF16),
        scratch_shapes=[pltpu.VMEM((l, LANES), BF16), pltpu.VMEM((lc, LANES), BF16),
                        pltpu.VMEM((2, nt, nq, nk), F32)],
        compiler_params=_params(est, ("parallel", "arbitrary")),
        name="nbr_attention",
    )(qkv, qkv, qkv, qkv_c, qkv_c, q_g, k_g, pieces)


def _ctx_attn_kernel(qkv_ref, qg_ref, kg_ref, o_ref, *, scale):
    head0 = lax.broadcasted_iota(jnp.int32, (1, LANES), 1) < HEAD_DIM
    npair = o_ref.shape[2] // LANES
    for p in range(npair):
        cols = lambda part: slice((part * npair + p) * LANES, (part * npair + p + 1) * LANES)
        kn = _pair_norm(qkv_ref[0, :, cols(1)].astype(F32), kg_ref[0], head0).astype(BF16)
        qn = _pair_norm(qkv_ref[0, :, cols(0)].astype(F32), qg_ref[0], head0) * scale
        v = qkv_ref[0, :, cols(2)]
        outs = []
        for h in range(2):
            sel = head0 if h == 0 else jnp.logical_not(head0)
            qh = jnp.where(sel, qn, 0.0).astype(BF16)
            outs.append(_softmax_pv([_dot_t(qh, kn)], [v]))
        o_ref[0, :, cols(0)] = jnp.where(head0, outs[0], outs[1]).astype(o_ref.dtype)


def _ctx_attention(qkv_c, q_g, k_g, idx):
    b, lc, d3 = qkv_c.shape
    d = d3 // 3
    est = 2 * lc * (d3 + d) * 2 + 16 * lc * lc * 4
    return pl.pallas_call(
        functools.partial(_ctx_attn_kernel, scale=float(HEAD_DIM ** -0.5)),
        grid=(b,),
        in_specs=[pl.BlockSpec((1, lc, d3), lambda i: (i, 0, 0)),
                  _layer_spec(q_g, idx), _layer_spec(k_g, idx)],
        out_specs=pl.BlockSpec((1, lc, d), lambda i: (i, 0, 0)),
        out_shape=jax.ShapeDtypeStruct((b, lc, d), BF16),
        compiler_params=_params(est, ("parallel",)),
        name="ctx_attention",
    )(qkv_c, q_g, k_g)


def _ffn_kernel(x_ref, xp_ref, xn_ref, g_ref, sh_ref, sc_ref, gate_ref, wu_ref, cw_ref, cb_ref, wd_ref,
                o_ref, h_ref, acc_ref, *, chunks):
    j = pl.program_id(1)
    tm = x_ref.shape[1]
    dff = wd_ref.shape[1]
    g, sh, sc = g_ref[0], sh_ref[0, 0, 0], sc_ref[0, 0, 0]
    keep_prev = (j > 0).astype(F32)
    keep_next = (j < pl.num_programs(1) - 1).astype(F32)
    h_ref[0:HALO, :] = (_rms_modulate(xp_ref[0], g, sh, sc) * keep_prev).astype(BF16)
    h_ref[HALO:HALO + tm, :] = _rms_modulate(x_ref[0], g, sh, sc).astype(BF16)
    h_ref[HALO + tm:, :] = (_rms_modulate(xn_ref[0], g, sh, sc) * keep_next).astype(BF16)
    ext = tm + 2 * HALO
    for c0, nc in chunks:
        ge = _dot(h_ref[...], wu_ref[0, :, c0:c0 + nc])
        val = _dot(h_ref[HALO:HALO + tm, :], wu_ref[0, :, dff + c0:dff + c0 + nc])
        gp = pltpu.roll(ge, 1, axis=0)[HALO:HALO + tm]
        gn = pltpu.roll(ge, ext - 1, axis=0)[HALO:HALO + tm]
        gc = ge[HALO:HALO + tm]
        cw = cw_ref[0, :, c0:c0 + nc]
        t = gp * cw[0:1] + gc * cw[1:2] + gn * cw[2:3] + cb_ref[0, :, c0:c0 + nc]
        ht = 0.5 * t
        act = ((ht + ht * jnp.tanh(ht)) * val).astype(BF16)
        part = _dot(act, wd_ref[0, c0:c0 + nc, :])
        if c0 == 0:
            acc_ref[...] = part
        else:
            acc_ref[...] += part
    o_ref[0] = x_ref[0] + gate_ref[0, 0, 0] * acc_ref[...]


def _ffn(x, norm_g, mods, layer, row, w_up, conv_w, conv_b, w_down):
    b, l, d = x.shape
    dff = w_down.shape[1]
    tm = min(l, 512)
    nt = l // tm
    hb = tm // HALO
    nhb = l // HALO
    chunks, c0 = [], 0
    while c0 < dff:
        nc = min(512, dff - c0)
        chunks.append((c0, nc))
        c0 += nc
    est = (4 * tm * d * 4 + 4 * HALO * d * 4 + 3 * d * dff * 2 + (tm + 2 * HALO) * d * 2 + tm * d * 4
           + 8 * (tm + 2 * HALO) * 512 * 4)
    return pl.pallas_call(
        functools.partial(_ffn_kernel, chunks=tuple(chunks)),
        grid=(b, nt),
        in_specs=[pl.BlockSpec((1, tm, d), lambda i, j: (i, j, 0)),
                  pl.BlockSpec((1, HALO, d), lambda i, j: (i, jnp.maximum(j * hb - 1, 0), 0)),
                  pl.BlockSpec((1, HALO, d), lambda i, j: (i, jnp.minimum((j + 1) * hb, nhb - 1), 0)),
                  _layer_spec(norm_g, layer),
                  _mod_spec(mods, layer, 3, row), _mod_spec(mods, layer, 4, row), _mod_spec(mods, layer, 5, row),
                  _layer_spec(w_up, layer), _layer_spec(conv_w, layer), _layer_spec(conv_b, layer),
                  _layer_spec(w_down, layer)],
        out_specs=pl.BlockSpec((1, tm, d), lambda i, j: (i, j, 0)),
        out_shape=jax.ShapeDtypeStruct((b, l, d), F32),
        scratch_shapes=[pltpu.VMEM((tm + 2 * HALO, d), BF16), pltpu.VMEM((tm, d), F32)],
        compiler_params=_params(est, ("parallel", "parallel")),
        name="conv_ffn",
    )(x, x, x, norm_g, mods, mods, mods, w_up, conv_w, conv_b, w_down)


def kernel(x, c, ctx, c_ctx, norm1_g, norm2_g, ada_w, ada_b, even_w_in, even_w_s, even_b_s, even_w_out,
           odd_w_qkv, odd_q_g, odd_k_g, odd_rpb, odd_w_o, ffn_w_up, ffn_conv_w, ffn_conv_b, ffn_w_down):
    b, l, d = x.shape
    depth = ada_w.shape[0]
    assert b + 1 <= MOD_ROWS

    cond = jnp.zeros((MOD_ROWS, d), F32).at[:b].set(c).at[b].set(c_ctx)
    mods = _ada_mods(cond, ada_w, ada_b)
    mods = mods.reshape(mods.shape[:3] + (1, d))
    lat, cx = None, b

    bf = lambda w: w.astype(BF16)
    w_in, w_s, w_out = bf(even_w_in), bf(even_w_s), bf(even_w_out)
    w_qkv, w_o = bf(odd_w_qkv), bf(odd_w_o)
    w_up, w_down = bf(ffn_w_up), bf(ffn_w_down)
    b_s = jnp.broadcast_to(even_b_s[..., None], even_b_s.shape + (GROUP_DIM,))
    n1, n2 = norm1_g[:, None, :], norm2_g[:, None, :]
    conv_b = ffn_conv_b[:, None, :]
    fw = N_GROUPS * GROUP_DIM

    cc, sc = _cos_sin(np.arange(GROUP_DIM), GROUP_DIM)
    csc = bf(jnp.asarray(np.concatenate([cc, -sc], axis=1)))
    pair = lambda v: jnp.tile(v, (1, LANES // HEAD_DIM))[:, None, :]
    q_g, k_g = pair(odd_q_g), pair(odd_k_g)
    pieces = _bias_pieces(odd_rpb)

    for layer in range(depth):
        last = layer == depth - 1
        i = layer // 2
        if layer % 2 == 0:
            p = _modmm(x, n1, mods, layer, lat, w_in, i, gelu_from=fw)
            x = _sgu_outproj(_fourier(p, csc), p, w_s, b_s, w_out, i, x, mods, layer, lat)
            if not last:
                p_c = _modmm(ctx, n1, mods, layer, cx, w_in, i, gelu_from=fw)
                ctx = _sgu_outproj(_fourier(p_c, csc), p_c, w_s, b_s, w_out, i, ctx, mods, layer, cx)
        else:
            qkv = _modmm(x, n1, mods, layer, lat, w_qkv, i, gelu_from=3 * d)
            qkv_c = _modmm(ctx, n1, mods, layer, cx, w_qkv, i, gelu_from=3 * d)
            x = _outproj(_nbr_attention(qkv, qkv_c, q_g, k_g, pieces, i), w_o, i, x, mods, layer, lat)
            if not last:
                ctx = _outproj(_ctx_attention(qkv_c, q_g, k_g, i), w_o, i, ctx, mods, layer, cx)

        x = _ffn(x, n2, mods, layer, lat, w_up, ffn_conv_w, conv_b, w_down)
        if not last:
            ctx = _ffn(ctx, n2, mods, layer, cx, w_up, ffn_conv_w, conv_b, w_down)
    return x
```

```python
import functools

import numpy as np
import jax
import jax.numpy as jnp
from jax import lax
from jax.experimental import pallas as pl
from jax.experimental.pallas import tpu as pltpu

F32 = jnp.float32
BF16 = jnp.bfloat16

GRID_W = 64
WIN_H = 8
WIN_W = 16
CHUNK = 128
N_GROUPS = 4
GROUP_DIM = 128
HEAD_DIM = 64
N_MOD = 6
EPS = 1e-6
NEG_INF = -1e30
LOG2E = float(np.log2(np.e))

LANES = 128
MOD_ROWS = 16
HALO = 16
Q_ROWS = 4
DFT_ROWS = 256
VMEM_CAP = 60 * 1024 * 1024


def _vmem_limit(nbytes):
    return int(min(max(nbytes * 5 // 4, 32 * 1024 * 1024), VMEM_CAP))


def _params(nbytes, semantics):
    return pltpu.CompilerParams(dimension_semantics=semantics, vmem_limit_bytes=_vmem_limit(nbytes))


def _resident(shape):
    nd = len(shape)
    return pl.BlockSpec(shape, lambda *_: (0,) * nd, pipeline_mode=pl.Buffered(1))


def _layer_spec(arr, idx):
    nd = arr.ndim
    return pl.BlockSpec((1,) + arr.shape[1:], lambda *_: (idx,) + (0,) * (nd - 1), pipeline_mode=pl.Buffered(1))


def _mod_spec(mods, layer, k, row):
    blk = (1, 1, 1, 1, mods.shape[-1])
    if row is None:
        return pl.BlockSpec(blk, lambda i, *_: (layer, k, i, 0, 0))
    return pl.BlockSpec(blk, lambda *_: (layer, k, row, 0, 0))


def _rms_modulate(x, g, sh, sc):
    ms = jnp.mean(x * x, axis=-1, keepdims=True)
    return (x * lax.rsqrt(ms + EPS) * g) * (1.0 + sc) + sh


def _gelu_tanh(x):
    c = np.float32(np.sqrt(2.0 / np.pi))
    return x * (0.5 * (1.0 + jnp.tanh(c * (x + 0.044715 * (x * x * x)))))


def _dot(a, b):
    return jnp.dot(a, b, preferred_element_type=F32)


def _dot_t(a, b):
    return lax.dot_general(a, b, (((1,), (1,)), ((), ())), preferred_element_type=F32)


def _ada_kernel(c_ref, w_ref, b_ref, o_ref):
    s = c_ref[...]
    s = s * (0.5 * (1.0 + jnp.tanh(0.5 * s)))
    o_ref[0, 0] = jnp.dot(s, w_ref[0], preferred_element_type=F32,
                          precision=lax.Precision.HIGHEST) + b_ref[0, 0]


def _ada_mods(cond, ada_w, ada_b):
    depth, d, n = ada_w.shape
    nm = n // d
    return pl.pallas_call(
        _ada_kernel,
        grid=(depth, nm),
        in_specs=[pl.BlockSpec((MOD_ROWS, d), lambda l, k: (0, 0)),
                  pl.BlockSpec((1, d, d), lambda l, k: (l, 0, k)),
                  pl.BlockSpec((1, 1, 1, d), lambda l, k: (l, k, 0, 0))],
        out_specs=pl.BlockSpec((1, 1, MOD_ROWS, d), lambda l, k: (l, k, 0, 0)),
        out_shape=jax.ShapeDtypeStruct((depth, nm, MOD_ROWS, d), F32),
        compiler_params=_params(2 * d * d * 4 + 8 * d * d * 2, ("parallel", "parallel")),
        name="ada_mods",
    )(cond, ada_w, ada_b.reshape(depth, nm, 1, d))


def _modmm_kernel(x_ref, g_ref, sh_ref, sc_ref, w_ref, o_ref, *, nc, gelu_from):
    h = _rms_modulate(x_ref[0], g_ref[0], sh_ref[0, 0, 0], sc_ref[0, 0, 0]).astype(BF16)
    n = w_ref.shape[2]
    for c0 in range(0, n, nc):
        a = _dot(h, w_ref[0, :, c0:c0 + nc])
        if c0 >= gelu_from:
            a = _gelu_tanh(a)
        o_ref[0, :, c0:c0 + nc] = a.astype(o_ref.dtype)


def _modmm(x, norm_g, mods, layer, row, w, widx, *, gelu_from):
    b, l, d = x.shape
    n = w.shape[2]
    tm = min(l, 512)
    nc = 512
    est = 2 * tm * d * 4 + d * n * 2 + 2 * tm * n * 2 + tm * d * 6 + tm * nc * 12
    return pl.pallas_call(
        functools.partial(_modmm_kernel, nc=nc, gelu_from=gelu_from),
        grid=(b, l // tm),
        in_specs=[pl.BlockSpec((1, tm, d), lambda i, j: (i, j, 0)),
                  _layer_spec(norm_g, layer), _mod_spec(mods, layer, 0, row), _mod_spec(mods, layer, 1, row),
                  _layer_spec(w, widx)],
        out_specs=pl.BlockSpec((1, tm, n), lambda i, j: (i, j, 0)),
        out_shape=jax.ShapeDtypeStruct((b, l, n), BF16),
        compiler_params=_params(est, ("parallel", "parallel")),
        name="modulate_proj",
    )(x, norm_g, mods, mods, w)


def _fourier_kernel(x_ref, csc_ref, rc_ref, rs_ref, kc_ref, ks_ref, o_ref, dft_ref, r_ref, *, l, scale, tr):
    @pl.when(pl.program_id(0) == 0)
    def _():
        nr = rc_ref.shape[0]
        for i in range(l // nr):
            kc, ks = kc_ref[i], ks_ref[i]
            dft_ref[i * nr:(i + 1) * nr, 0:l] = (kc * rc_ref[...] - ks * rs_ref[...]).astype(BF16)
            dft_ref[i * nr:(i + 1) * nr, l:2 * l] = (ks * rc_ref[...] + kc * rs_ref[...]).astype(BF16)

    gd = GROUP_DIM
    for g in range(N_GROUPS):
        for r0 in range(0, l, tr):
            z = _dot(x_ref[0, r0:r0 + tr, g * gd:(g + 1) * gd], csc_ref[...])
            r_ref[r0:r0 + tr, g * gd:(g + 1) * gd] = z[:, :gd].astype(BF16)
            r_ref[l + r0:l + r0 + tr, g * gd:(g + 1) * gd] = z[:, gd:].astype(BF16)
    for r0 in range(0, l, tr):
        y = _dot(dft_ref[r0:r0 + tr, :], r_ref[...]) * scale
        o_ref[0, r0:r0 + tr, :] = y.astype(o_ref.dtype)


def _cos_sin(rows, n):
    ang = ((rows.astype(np.int64)[:, None] * np.arange(n, dtype=np.int64)[None, :]) % n) * (2.0 * np.pi / n)
    return np.cos(ang).astype(np.float32), np.sin(ang).astype(np.float32)


def _fourier(p, csc):
    b, l, _ = p.shape
    w = N_GROUPS * GROUP_DIM
    tr = min(l, 512)
    nr = min(l, DFT_ROWS)
    rc, rs = _cos_sin(np.arange(nr), l)
    kc, ks = _cos_sin(np.arange(0, l, nr), l)
    kc, ks = kc[:, None, :], ks[:, None, :]
    scale = float(1.0 / np.sqrt(l * GROUP_DIM))
    est = 2 * l * w * 2 * 2 + l * 2 * l * 2 + 2 * l * w * 2 + tr * w * 8 + 2 * nr * l * 4 + 4 * nr * l * 4
    return pl.pallas_call(
        functools.partial(_fourier_kernel, l=l, scale=scale, tr=tr),
        grid=(b,),
        in_specs=[pl.BlockSpec((1, l, w), lambda i: (i, 0, 0)),
                  _resident(csc.shape), _resident(rc.shape), _resident(rs.shape),
                  _resident(kc.shape), _resident(ks.shape)],
        out_specs=pl.BlockSpec((1, l, w), lambda i: (i, 0, 0)),
        out_shape=jax.ShapeDtypeStruct((b, l, w), BF16),
        scratch_shapes=[pltpu.VMEM((l, 2 * l), BF16), pltpu.VMEM((2 * l, w), BF16)],
        compiler_params=_params(est, ("arbitrary",)),
        name="fourier_mix",
    )(p, csc, rc, rs, kc, ks)


def _sgu_outproj_kernel(a_ref, u_ref, v_ref, ws_ref, bs_ref, w_ref, x_ref, gate_ref, o_ref, b_scr):
    tm = a_ref.shape[1]
    gd = GROUP_DIM
    for g in range(N_GROUPS):
        v = v_ref[0, :, g * gd:(g + 1) * gd].astype(F32)
        vn = (v * lax.rsqrt(jnp.mean(v * v, axis=-1, keepdims=True) + EPS)).astype(BF16)
        for r0 in range(0, tm, CHUNK):
            s = _dot(ws_ref[0, g], vn[r0:r0 + CHUNK]) + bs_ref[0, g]
            u = u_ref[0, r0:r0 + CHUNK, g * gd:(g + 1) * gd].astype(F32)
            b_scr[r0:r0 + CHUNK, g * gd:(g + 1) * gd] = (u * s).astype(BF16)
    wa = a_ref.shape[2]
    y = _dot(a_ref[0], w_ref[0, 0:wa, :]) + _dot(b_scr[...], w_ref[0, wa:, :])
    o_ref[0] = x_ref[0] + gate_ref[0, 0, 0] * y


def _sgu_outproj(a, p, w_s, b_s, w_out, widx, x, mods, layer, row):
    b, l, d = x.shape
    wa = a.shape[2]
    wb = N_GROUPS * GROUP_DIM
    tm = min(l, 512)
    est = 2 * tm * (wa + 2 * wb) * 2 + d * d * 2 + 4 * tm * d * 4 + tm * wb * 2 + tm * d * 8
    return pl.pallas_call(
        _sgu_outproj_kernel,
        grid=(b, l // tm),
        in_specs=[pl.BlockSpec((1, tm, wa), lambda i, j: (i, j, 0)),
                  pl.BlockSpec((1, tm, wb), lambda i, j: (i, j, wa // wb)),
                  pl.BlockSpec((1, tm, wb), lambda i, j: (i, j, wa // wb + 1)),
                  _layer_spec(w_s, widx), _layer_spec(b_s, widx), _layer_spec(w_out, widx),
                  pl.BlockSpec((1, tm, d), lambda i, j: (i, j, 0)),
                  _mod_spec(mods, layer, 2, row)],
        out_specs=pl.BlockSpec((1, tm, d), lambda i, j: (i, j, 0)),
        out_shape=jax.ShapeDtypeStruct((b, l, d), F32),
        scratch_shapes=[pltpu.VMEM((tm, wb), BF16)],
        compiler_params=_params(est, ("parallel", "parallel")),
        name="sgu_out_proj",
    )(a, p, p, w_s, b_s, w_out, x, mods)


def _outproj_kernel(a_ref, w_ref, x_ref, gate_ref, o_ref):
    o_ref[0] = x_ref[0] + gate_ref[0, 0, 0] * _dot(a_ref[0], w_ref[0])


def _outproj(a, w_out, widx, x, mods, layer, row):
    b, l, d = x.shape
    k = a.shape[2]
    tm = min(l, 512)
    est = 2 * tm * k * 2 + k * d * 2 + 4 * tm * d * 4 + tm * d * 8
    return pl.pallas_call(
        _outproj_kernel,
        grid=(b, l // tm),
        in_specs=[pl.BlockSpec((1, tm, k), lambda i, j: (i, j, 0)),
                  _layer_spec(w_out, widx),
                  pl.BlockSpec((1, tm, d), lambda i, j: (i, j, 0)),
                  _mod_spec(mods, layer, 2, row)],
        out_specs=pl.BlockSpec((1, tm, d), lambda i, j: (i, j, 0)),
        out_shape=jax.ShapeDtypeStruct((b, l, d), F32),
        compiler_params=_params(est, ("parallel", "parallel")),
        name="out_proj",
    )(a, w_out, x, mods)


def _pair_norm(x, gain, head0):
    sq = x * x
    s0 = jnp.sum(jnp.where(head0, sq, 0.0), axis=-1, keepdims=True)
    s1 = jnp.sum(jnp.where(head0, 0.0, sq), axis=-1, keepdims=True)
    inv = 1.0 / HEAD_DIM
    rinv = jnp.where(head0, lax.rsqrt(s0 * inv + EPS), lax.rsqrt(s1 * inv + EPS))
    return x * rinv * gain


def _ones_in_other_head(v, head0):
    vf = v.astype(F32)
    return jnp.where(head0, vf, 1.0).astype(BF16), jnp.where(head0, 1.0, vf).astype(BF16)


def _pair_attend(qn, head0, keys, biases, values):
    qs = [jnp.where(head0, qn, 0.0).astype(BF16), jnp.where(head0, 0.0, qn).astype(BF16)]
    ss = []
    for h in range(2):
        parts = [_dot_t(qs[h], k) if b is None else _dot_t(qs[h], k) + b[h] for k, b in zip(keys, biases)]
        ss.append(parts[0] if len(parts) == 1 else jnp.concatenate(parts, axis=-1))
    es = [jnp.exp2((s - s.max(axis=-1, keepdims=True)).astype(BF16)) for s in ss]
    outs = []
    for h in range(2):
        o, c0 = None, 0
        for v in values[h]:
            part = _dot(es[h][:, c0:c0 + v.shape[0]], v)
            o = part if o is None else o + part
            c0 += v.shape[0]
        outs.append(o)
    num = jnp.where(head0, outs[0], outs[1])
    den = pltpu.roll(jnp.where(head0, outs[1], outs[0]), HEAD_DIM, axis=1)
    return num / den


def _toeplitz(row, col0):
    return pltpu.roll(jnp.broadcast_to(row, (GRID_W, LANES)), (col0 - (WIN_W - 1)) % LANES, axis=1,
                      stride=1, stride_axis=0)


def _nbr_attn_kernel(q_ref, k_ref, v_ref, kc_ref, vc_ref, qg_ref, kg_ref, rpb_ref, o_ref,
                     kn_ref, kcn_ref, v1_ref, vc1_ref, bias_ref, *, blocks, dplan, qscale):
    head0 = lax.broadcasted_iota(jnp.int32, (1, LANES), 1) < HEAD_DIM
    w = GRID_W

    @pl.when(pl.program_id(1) == 0)
    def _():
        qc = lax.broadcasted_iota(jnp.int32, (w, LANES), 0)
        lane = lax.broadcasted_iota(jnp.int32, (w, LANES), 1)
        kc = lane & (w - 1)
        cs = jnp.clip(qc - WIN_W // 2, 0, w - WIN_W)
        in_cols = (kc >= cs) & (kc < cs + WIN_W)
        left = lane < w
        outside = 2 * WIN_H - 1
        for h in range(2):
            rows_d = [rpb_ref[0, h, d:d + 1, :] * LOG2E for d in range(outside)]
            lo = [_toeplitz(r, 0) for r in rows_d] + [jnp.full((w, LANES), NEG_INF, F32)]
            hi = [_toeplitz(r, w) for r in rows_d] + [lo[outside]]
            for t, rows in enumerate(dplan):
                for rq, ds in enumerate(rows):
                    for kp in range(len(ds) // 2):
                        piece = jnp.where(in_cols, jnp.where(left, lo[ds[2 * kp]], hi[ds[2 * kp + 1]]), NEG_INF)
                        bias_ref[h, t, rq * w:(rq + 1) * w, 2 * kp * w:(2 * kp + 2) * w] = piece

    l = k_ref.shape[1]
    tr = 512
    for r0 in range(0, l, tr):
        kn_ref[r0:r0 + tr, :] = _pair_norm(k_ref[0, r0:r0 + tr, :].astype(F32), kg_ref[0], head0).astype(BF16)
        v1_ref[0, r0:r0 + tr, :], v1_ref[1, r0:r0 + tr, :] = _ones_in_other_head(v_ref[0, r0:r0 + tr, :], head0)
    kcn_ref[...] = _pair_norm(kc_ref[0].astype(F32), kg_ref[0], head0).astype(BF16)
    vc1_ref[0], vc1_ref[1] = _ones_in_other_head(vc_ref[0], head0)
    nq = Q_ROWS * w
    for j, (k0, nk, t) in enumerate(blocks):
        qn = _pair_norm(q_ref[0, j * nq:(j + 1) * nq, :].astype(F32), qg_ref[0], head0) * qscale
        o = _pair_attend(qn, head0, [kn_ref[k0:k0 + nk, :], kcn_ref[...]],
                         [(bias_ref[0, t], bias_ref[1, t]), None],
                         [[v1_ref[h, k0:k0 + nk, :], vc1_ref[h]] for h in range(2)])
        o_ref[0, j * nq:(j + 1) * nq, :] = o.astype(o_ref.dtype)


def _nbr_plan(rows):
    kh = min(WIN_H, rows)
    kr = min(Q_ROWS + kh, rows)
    assert kr % 2 == 0 and rows % Q_ROWS == 0
    outside = 2 * WIN_H - 1
    blocks, types = [], []
    for j in range(rows // Q_ROWS):
        ks = int(np.clip(Q_ROWS * j - kh // 2, 0, rows - kr))
        d = [[outside] * kr for _ in range(Q_ROWS)]
        for rq in range(Q_ROWS):
            r = Q_ROWS * j + rq
            rs = int(np.clip(r - kh // 2, 0, rows - kh))
            for i in range(kh):
                d[rq][rs + i - ks] = rs + i - r + (WIN_H - 1)
        d = tuple(tuple(row) for row in d)
        if d not in types:
            types.append(d)
        blocks.append((ks * GRID_W, kr * GRID_W, types.index(d)))
    return tuple(blocks), tuple(types)


def _nbr_attention(qkv, qkv_c, q_g, k_g, rpb, idx):
    b, l, d3 = qkv.shape
    d = d3 // 3
    lc = qkv_c.shape[1]
    npair = d // LANES
    blocks, dplan = _nbr_plan(l // GRID_W)
    nt = len(dplan)
    nq, nk = Q_ROWS * GRID_W, blocks[0][1]
    est = (2 * 4 * l * LANES * 2 + 2 * 2 * lc * LANES * 2 + 2 * nt * nq * nk * 4 + 3 * (l + lc) * LANES * 2
           + 6 * nq * (nk + lc) * 4)
    return pl.pallas_call(
        functools.partial(_nbr_attn_kernel, blocks=blocks, dplan=dplan, qscale=float(HEAD_DIM ** -0.5) * LOG2E),
        grid=(npair, b),
        in_specs=[pl.BlockSpec((1, l, LANES), lambda p, i: (i, 0, p)),
                  pl.BlockSpec((1, l, LANES), lambda p, i: (i, 0, npair + p)),
                  pl.BlockSpec((1, l, LANES), lambda p, i: (i, 0, 2 * npair + p)),
                  pl.BlockSpec((1, lc, LANES), lambda p, i: (i, 0, npair + p)),
                  pl.BlockSpec((1, lc, LANES), lambda p, i: (i, 0, 2 * npair + p)),
                  _layer_spec(q_g, idx), _layer_spec(k_g, idx),
                  pl.BlockSpec((1, 2) + rpb.shape[2:], lambda p, i: (idx, p, 0, 0))],
        out_specs=pl.BlockSpec((1, l, LANES), lambda p, i: (i, 0, p)),
        out_shape=jax.ShapeDtypeStruct((b, l, d), BF16),
        scratch_shapes=[pltpu.VMEM((l, LANES), BF16), pltpu.VMEM((lc, LANES), BF16),
                        pltpu.VMEM((2, l, LANES), BF16), pltpu.VMEM((2, lc, LANES), BF16),
                        pltpu.VMEM((2, nt, nq, nk), F32)],
        compiler_params=_params(est, ("parallel", "arbitrary")),
        name="nbr_attention",
    )(qkv, qkv, qkv, qkv_c, qkv_c, q_g, k_g, rpb)


def _ctx_attn_kernel(qkv_ref, qg_ref, kg_ref, o_ref, *, qscale):
    head0 = lax.broadcasted_iota(jnp.int32, (1, LANES), 1) < HEAD_DIM
    npair = o_ref.shape[2] // LANES
    for p in range(npair):
        cols = lambda part: slice((part * npair + p) * LANES, (part * npair + p + 1) * LANES)
        kn = _pair_norm(qkv_ref[0, :, cols(1)].astype(F32), kg_ref[0], head0).astype(BF16)
        qn = _pair_norm(qkv_ref[0, :, cols(0)].astype(F32), qg_ref[0], head0) * qscale
        v1 = _ones_in_other_head(qkv_ref[0, :, cols(2)], head0)
        o_ref[0, :, cols(0)] = _pair_attend(qn, head0, [kn], [None], [[v1[0]], [v1[1]]]).astype(o_ref.dtype)


def _ctx_attention(qkv_c, q_g, k_g, idx):
    b, lc, d3 = qkv_c.shape
    d = d3 // 3
    est = 2 * lc * (d3 + d) * 2 + 16 * lc * lc * 4
    return pl.pallas_call(
        functools.partial(_ctx_attn_kernel, qscale=float(HEAD_DIM ** -0.5) * LOG2E),
        grid=(b,),
        in_specs=[pl.BlockSpec((1, lc, d3), lambda i: (i, 0, 0)),
                  _layer_spec(q_g, idx), _layer_spec(k_g, idx)],
        out_specs=pl.BlockSpec((1, lc, d), lambda i: (i, 0, 0)),
        out_shape=jax.ShapeDtypeStruct((b, lc, d), BF16),
        compiler_params=_params(est, ("parallel",)),
        name="ctx_attention",
    )(qkv_c, q_g, k_g)


def _ffn_kernel(x_ref, xp_ref, xn_ref, g_ref, sh_ref, sc_ref, gate_ref, wu_ref, cw_ref, cb_ref, wd_ref,
                o_ref, h_ref, acc_ref, *, chunks):
    j = pl.program_id(1)
    tm = x_ref.shape[1]
    dff = wd_ref.shape[1]
    g, sh, sc = g_ref[0], sh_ref[0, 0, 0], sc_ref[0, 0, 0]
    keep_prev = (j > 0).astype(F32)
    keep_next = (j < pl.num_programs(1) - 1).astype(F32)
    h_ref[0:HALO, :] = (_rms_modulate(xp_ref[0], g, sh, sc) * keep_prev).astype(BF16)
    h_ref[HALO:HALO + tm, :] = _rms_modulate(x_ref[0], g, sh, sc).astype(BF16)
    h_ref[HALO + tm:, :] = (_rms_modulate(xn_ref[0], g, sh, sc) * keep_next).astype(BF16)
    ext = tm + 2 * HALO

    def up(c0, nc):
        return (_dot(h_ref[...], wu_ref[0, :, c0:c0 + nc]),
                _dot(h_ref[HALO:HALO + tm, :], wu_ref[0, :, dff + c0:dff + c0 + nc]))

    nxt = up(*chunks[0])
    for ci, (c0, nc) in enumerate(chunks):
        ge, val = nxt
        if ci + 1 < len(chunks):
            nxt = up(*chunks[ci + 1])
        gp = pltpu.roll(ge, 1, axis=0)[HALO:HALO + tm]
        gn = pltpu.roll(ge, ext - 1, axis=0)[HALO:HALO + tm]
        gc = ge[HALO:HALO + tm]
        cw = cw_ref[0, :, c0:c0 + nc]
        t = gp * cw[0:1] + gc * cw[1:2] + gn * cw[2:3] + cb_ref[0, :, c0:c0 + nc]
        ht = 0.5 * t
        act = ((ht + ht * jnp.tanh(ht)) * val).astype(BF16)
        part = _dot(act, wd_ref[0, c0:c0 + nc, :])
        if c0 == 0:
            acc_ref[...] = part
        else:
            acc_ref[...] += part
    o_ref[0] = x_ref[0] + gate_ref[0, 0, 0] * acc_ref[...]


def _ffn(x, norm_g, mods, layer, row, w_up, conv_w, conv_b, w_down):
    b, l, d = x.shape
    dff = w_down.shape[1]
    tm = min(l, 512)
    nt = l // tm
    hb = tm // HALO
    nhb = l // HALO
    chunks, c0 = [], 0
    while c0 < dff:
        nc = min(256, dff - c0)
        chunks.append((c0, nc))
        c0 += nc
    est = (4 * tm * d * 4 + 4 * HALO * d * 4 + 3 * d * dff * 2 + (tm + 2 * HALO) * d * 2 + tm * d * 4
           + 8 * (tm + 2 * HALO) * 512 * 4)
    return pl.pallas_call(
        functools.partial(_ffn_kernel, chunks=tuple(chunks)),
        grid=(b, nt),
        in_specs=[pl.BlockSpec((1, tm, d), lambda i, j: (i, j, 0)),
                  pl.BlockSpec((1, HALO, d), lambda i, j: (i, jnp.maximum(j * hb - 1, 0), 0)),
                  pl.BlockSpec((1, HALO, d), lambda i, j: (i, jnp.minimum((j + 1) * hb, nhb - 1), 0)),
                  _layer_spec(norm_g, layer),
                  _mod_spec(mods, layer, 3, row), _mod_spec(mods, layer, 4, row), _mod_spec(mods, layer, 5, row),
                  _layer_spec(w_up, layer), _layer_spec(conv_w, layer), _layer_spec(conv_b, layer),
                  _layer_spec(w_down, layer)],
        out_specs=pl.BlockSpec((1, tm, d), lambda i, j: (i, j, 0)),
        out_shape=jax.ShapeDtypeStruct((b, l, d), F32),
        scratch_shapes=[pltpu.VMEM((tm + 2 * HALO, d), BF16), pltpu.VMEM((tm, d), F32)],
        compiler_params=_params(est, ("parallel", "parallel")),
        name="conv_ffn",
    )(x, x, x, norm_g, mods, mods, mods, w_up, conv_w, conv_b, w_down)


def kernel(x, c, ctx, c_ctx, norm1_g, norm2_g, ada_w, ada_b, even_w_in, even_w_s, even_b_s, even_w_out,
           odd_w_qkv, odd_q_g, odd_k_g, odd_rpb, odd_w_o, ffn_w_up, ffn_conv_w, ffn_conv_b, ffn_w_down):
    b, l, d = x.shape
    depth = ada_w.shape[0]
    assert b + 1 <= MOD_ROWS

    cond = jnp.zeros((MOD_ROWS, d), F32).at[:b].set(c).at[b].set(c_ctx)
    mods = _ada_mods(cond, ada_w, ada_b)
    mods = mods.reshape(mods.shape[:3] + (1, d))
    lat, cx = None, b

    bf = lambda w: w.astype(BF16)
    w_in, w_s, w_out = bf(even_w_in), bf(even_w_s), bf(even_w_out)
    w_qkv, w_o = bf(odd_w_qkv), bf(odd_w_o)
    w_up, w_down = bf(ffn_w_up), bf(ffn_w_down)
    b_s = jnp.broadcast_to(even_b_s[..., None], even_b_s.shape + (GROUP_DIM,))
    n1, n2 = norm1_g[:, None, :], norm2_g[:, None, :]
    conv_b = ffn_conv_b[:, None, :]
    fw = N_GROUPS * GROUP_DIM

    cc, sc = _cos_sin(np.arange(GROUP_DIM), GROUP_DIM)
    csc = bf(jnp.asarray(np.concatenate([cc, -sc], axis=1)))
    pair = lambda v: jnp.tile(v, (1, LANES // HEAD_DIM))[:, None, :]
    q_g, k_g = pair(odd_q_g), pair(odd_k_g)
    nd, nw = odd_rpb.shape[2:]
    rpb = jnp.pad(odd_rpb, ((0, 0), (0, 0), (0, -nd % 8), (0, LANES - nw)))

    for layer in range(depth):
        last = layer == depth - 1
        i = layer // 2
        if layer % 2 == 0:
            p = _modmm(x, n1, mods, layer, lat, w_in, i, gelu_from=fw)
            x = _sgu_outproj(_fourier(p, csc), p, w_s, b_s, w_out, i, x, mods, layer, lat)
            if not last:
                p_c = _modmm(ctx, n1, mods, layer, cx, w_in, i, gelu_from=fw)
                ctx = _sgu_outproj(_fourier(p_c, csc), p_c, w_s, b_s, w_out, i, ctx, mods, layer, cx)
        else:
            qkv = _modmm(x, n1, mods, layer, lat, w_qkv, i, gelu_from=3 * d)
            qkv_c = _modmm(ctx, n1, mods, layer, cx, w_qkv, i, gelu_from=3 * d)
            x = _outproj(_nbr_attention(qkv, qkv_c, q_g, k_g, rpb, i), w_o, i, x, mods, layer, lat)
            if not last:
                ctx = _outproj(_ctx_attention(qkv_c, q_g, k_g, i), w_o, i, ctx, mods, layer, cx)

        x = _ffn(x, n2, mods, layer, lat, w_up, ffn_conv_w, conv_b, w_down)
        if not last:
            ctx = _ffn(ctx, n2, mods, layer, cx, w_up, ffn_conv_w, conv_b, w_down)
    return x
```
